```python
import jax
import jax.numpy as jnp
from jax import lax
import numpy as np

D_MODEL = 1024
BATCH = 8
SEQ = 2048
DEPTH = 2
DEC_BATCH = 128
DEC_SEQ = 4
PAST_LEN = 16384
PAGE_SIZE = 128

D_CONV = D_MODEL
CONV_K = 3
RET_HEADS = 4
RET_QK_W = D_MODEL
RET_V_W = 2 * D_MODEL
RET_DK = RET_QK_W // RET_HEADS
RET_DV = RET_V_W // RET_HEADS
RET_CHUNK = 128
ROPE_BASE = 10000.0
N_GROUPS = 4
EXPERTS_PER_GROUP = 8
N_EXPERTS = N_GROUPS * EXPERTS_PER_GROUP
TOP_K = 2
D_EXPERT = D_MODEL // 2
MOE_BLOCK = 128
NORM_EPS = 1e-6
PROJ_W = 3 * D_CONV + 2 * RET_QK_W + 2 * RET_V_W + 2 * D_MODEL
SPLIT_POINTS = [D_CONV, 2 * D_CONV, 3 * D_CONV,
                3 * D_CONV + RET_QK_W, 3 * D_CONV + 2 * RET_QK_W,
                3 * D_CONV + 2 * RET_QK_W + RET_V_W, 3 * D_CONV + 2 * RET_QK_W + 2 * RET_V_W,
                3 * D_CONV + 2 * RET_QK_W + 2 * RET_V_W + D_MODEL]

kernel_name = 'hybrid_conv_retention_hiermoe_step'


def rms_norm(x, g):
    xf = x.astype(jnp.float32)
    y = xf * lax.rsqrt(jnp.mean(xf * xf, axis=-1, keepdims=True) + NORM_EPS)
    return (y * g.astype(jnp.float32)).astype(x.dtype)


def rotary(t, pos):
    half = t.shape[-1] // 2
    inv = ROPE_BASE ** (-jnp.arange(half, dtype=jnp.float32) / half)
    ang = pos.astype(jnp.float32)[:, None] * inv[None, :]
    cos = jnp.cos(ang)[None, :, None, :]
    sin = jnp.sin(ang)[None, :, None, :]
    tf = t.astype(jnp.float32)
    t1, t2 = tf[..., :half], tf[..., half:]
    return jnp.concatenate([t1 * cos - t2 * sin, t1 * sin + t2 * cos], axis=-1).astype(t.dtype)


def short_conv(u, buf, w):
    L = u.shape[1]
    ext = jnp.concatenate([buf.astype(u.dtype), u], axis=1)
    y = ext[:, 0:L] * w[0]
    for j in range(1, CONV_K):
        y = y + ext[:, j:j + L] * w[j]
    return y, ext[:, L:]


def retention(q, k, v, s0, chunk):
    B, L, H, _ = q.shape
    n = L // chunk
    log_g = jnp.log(1.0 - 2.0 ** (-5.0 - jnp.arange(H, dtype=jnp.float32)))
    idx = jnp.arange(chunk, dtype=jnp.float32)
    rel = idx[:, None] - idx[None, :]
    dmat = jnp.where(rel >= 0, jnp.exp(jnp.maximum(rel, 0.0)[None] * log_g[:, None, None]), 0.0)
    q_dec = jnp.exp((idx[:, None] + 1.0) * log_g[None, :])
    k_dec = jnp.exp((chunk - 1.0 - idx[:, None]) * log_g[None, :])
    c_dec = jnp.exp(chunk * log_g)

    def blocks(t):
        t = t.astype(jnp.float32).reshape(B, n, chunk, H, t.shape[-1])
        return jnp.moveaxis(t, 1, 0)

    def step(s, inp):
        qc, kc, vc = inp
        scores = jnp.einsum('bihd,bjhd->bhij', qc, kc) * dmat
        intra = jnp.einsum('bhij,bjhe->bihe', scores, vc)
        cross = jnp.einsum('bihd,bhde->bihe', qc * q_dec[None, :, :, None], s)
        s_new = s * c_dec[None, :, None, None] + jnp.einsum(
            'bjhd,bjhe->bhde', kc * k_dec[None, :, :, None], vc)
        return s_new, intra + cross

    s_fin, out = lax.scan(step, s0.astype(jnp.float32), (blocks(q), blocks(k), blocks(v)))
    out = jnp.moveaxis(out, 0, 1).reshape(B, L, H, -1)
    return out, s_fin


def hybrid_mixer(h, conv_buf, ret_state, pos, w_in, conv_w, w_conv_out, w_ret_out, w_o):
    B, L, _ = h.shape
    proj = h @ w_in
    cb, cc, ch, q, k, v, g, ga, gb = jnp.split(proj, SPLIT_POINTS, axis=-1)
    conv_out, new_buf = short_conv(cc * ch, conv_buf, conv_w)
    y_a = (cb * conv_out) @ w_conv_out
    q = rotary(q.reshape(B, L, RET_HEADS, RET_DK), pos)
    k = rotary(k.reshape(B, L, RET_HEADS, RET_DK), pos) * (RET_DK ** -0.5)
    v = v.reshape(B, L, RET_HEADS, RET_DV)
    chunk = RET_CHUNK if L % RET_CHUNK == 0 else L
    o, new_state = retention(q, k, v, ret_state, chunk)
    o = o * lax.rsqrt(jnp.mean(o * o, axis=-1, keepdims=True) + NORM_EPS)
    o = jax.nn.silu(g) * o.reshape(B, L, RET_V_W).astype(h.dtype)
    y_b = o @ w_ret_out
    merged = jax.nn.sigmoid(ga) * y_a + jax.nn.sigmoid(gb) * y_b
    return merged @ w_o, new_buf, new_state.astype(h.dtype)


def hier_moe(h, w_group, b_group, w_router, b_router, w_gate, w_up, w_down):
    B, L, D = h.shape
    T = B * L
    x2 = h.reshape(T, D)
    g_logit = (x2 @ w_group).astype(jnp.float32) + b_group.astype(jnp.float32)
    g_top = jnp.argmax(g_logit, axis=-1)
    p_group = jnp.take_along_axis(jax.nn.softmax(g_logit, axis=-1), g_top[:, None], axis=1)
    e_logit = ((x2 @ w_router).astype(jnp.float32) + b_router.astype(jnp.float32)).reshape(
        T, N_GROUPS, EXPERTS_PER_GROUP)
    e_in_group = jnp.take_along_axis(e_logit, g_top[:, None, None], axis=1)[:, 0]
    top_v, top_i = lax.top_k(e_in_group, TOP_K)
    gate = p_group * jax.nn.softmax(top_v, axis=-1)
    expert_id = (g_top[:, None] * EXPERTS_PER_GROUP + top_i).astype(jnp.int32)
    A = T * TOP_K
    flat_e = expert_id.reshape(A)
    flat_w = gate.reshape(A)
    order = jnp.argsort(flat_e).astype(jnp.int32)
    sorted_e = flat_e[order]
    counts = jnp.bincount(flat_e, length=N_EXPERTS).astype(jnp.int32)
    padded = (counts + MOE_BLOCK - 1) // MOE_BLOCK * MOE_BLOCK
    start_sorted = jnp.cumsum(counts) - counts
    end_pad = jnp.cumsum(padded)
    start_pad = end_pad - padded
    dest = start_pad[sorted_e] + jnp.arange(A, dtype=jnp.int32) - start_sorted[sorted_e]
    n_blocks = -(-A // MOE_BLOCK) + N_EXPERTS
    n_slots = n_blocks * MOE_BLOCK
    tok_sorted = order // TOP_K
    slot_tok = jnp.full((n_slots,), T, jnp.int32).at[dest].set(tok_sorted)
    x_pad = jnp.concatenate([x2, jnp.zeros((1, D), x2.dtype)], axis=0)[slot_tok]
    x_pad = x_pad.reshape(n_blocks, MOE_BLOCK, D)
    block_expert = jnp.minimum(
        jnp.searchsorted(end_pad, jnp.arange(n_blocks, dtype=jnp.int32) * MOE_BLOCK, side='right'),
        N_EXPERTS - 1)

    def expert_block(args):
        xb, e = args
        return (jax.nn.silu(xb @ w_gate[e]) * (xb @ w_up[e])) @ w_down[e]

    y_slots = lax.map(expert_block, (x_pad, block_expert)).reshape(n_slots, D)
    contrib = y_slots[dest] * flat_w[order][:, None].astype(x2.dtype)
    out = jnp.zeros((T, D), x2.dtype).at[tok_sorted].add(contrib)
    return out.reshape(B, L, D)


def trunk(x, conv_state, ret_state, pos, norm_mix_g, w_in, conv_w, w_conv_out, w_ret_out, w_o,
          norm_ffn_g, w_group, b_group, w_router, b_router, w_gate, w_up, w_down, norm_final_g):
    conv_new = []
    ret_new = []
    for l in range(DEPTH):
        m, cbuf, rstate = hybrid_mixer(rms_norm(x, norm_mix_g[l]), conv_state[l], ret_state[l], pos,
                                       w_in[l], conv_w[l], w_conv_out[l], w_ret_out[l], w_o[l])
        x = x + m
        x = x + hier_moe(rms_norm(x, norm_ffn_g[l]), w_group[l], b_group[l], w_router[l], b_router[l],
                         w_gate[l], w_up[l], w_down[l])
        conv_new.append(cbuf)
        ret_new.append(rstate)
    return rms_norm(x, norm_final_g), jnp.stack(conv_new), jnp.stack(ret_new)


def setup_inputs(seed: int = 0) -> dict:
    key = jax.random.key(seed)
    ks = jax.random.split(key, 20)

    def nrm(k, shape, scale):
        return jax.random.normal(k, shape, jnp.float32) * scale

    return {
        'x_prompt': nrm(ks[0], (BATCH, SEQ, D_MODEL), 1.0),
        'x_sample': nrm(ks[1], (DEC_BATCH, DEC_SEQ, D_MODEL), 1.0),
        'state_conv': nrm(ks[2], (DEPTH, DEC_BATCH, CONV_K - 1, D_CONV), 1.0),
        'state_ret': nrm(ks[3], (DEPTH, DEC_BATCH, RET_HEADS, RET_DK, RET_DV), 0.5),
        'norm_mix_g': 1.0 + nrm(ks[4], (DEPTH, D_MODEL), 0.05),
        'w_in': nrm(ks[5], (DEPTH, D_MODEL, PROJ_W), D_MODEL ** -0.5),
        'conv_w': nrm(ks[6], (DEPTH, CONV_K, D_CONV), CONV_K ** -0.5),
        'w_conv_out': nrm(ks[7], (DEPTH, D_CONV, D_MODEL), D_CONV ** -0.5),
        'w_ret_out': nrm(ks[8], (DEPTH, RET_V_W, D_MODEL), RET_V_W ** -0.5),
        'w_o': nrm(ks[9], (DEPTH, D_MODEL, D_MODEL), D_MODEL ** -0.5),
        'norm_ffn_g': 1.0 + nrm(ks[10], (DEPTH, D_MODEL), 0.05),
        'w_group': nrm(ks[11], (DEPTH, D_MODEL, N_GROUPS), D_MODEL ** -0.5),
        'b_group': nrm(ks[12], (DEPTH, N_GROUPS), 0.01),
        'w_router': nrm(ks[13], (DEPTH, D_MODEL, N_EXPERTS), D_MODEL ** -0.5),
        'b_router': nrm(ks[14], (DEPTH, N_EXPERTS), 0.01),
        'w_gate': nrm(ks[15], (DEPTH, N_EXPERTS, D_MODEL, D_EXPERT), D_MODEL ** -0.5),
        'w_up': nrm(ks[16], (DEPTH, N_EXPERTS, D_MODEL, D_EXPERT), D_MODEL ** -0.5),
        'w_down': nrm(ks[17], (DEPTH, N_EXPERTS, D_EXPERT, D_MODEL), D_EXPERT ** -0.5),
        'norm_final_g': 1.0 + nrm(ks[18], (D_MODEL,), 0.05),
    }


def reference(x_prompt, x_sample, state_conv, state_ret, norm_mix_g, w_in, conv_w, w_conv_out,
              w_ret_out, w_o, norm_ffn_g, w_group, b_group, w_router, b_router, w_gate, w_up,
              w_down, norm_final_g):
    bp, lp = x_prompt.shape[0], x_prompt.shape[1]
    conv0 = jnp.zeros((DEPTH, bp, CONV_K - 1, D_CONV), x_prompt.dtype)
    ret0 = jnp.zeros((DEPTH, bp, RET_HEADS, RET_DK, RET_DV), x_prompt.dtype)
    pos_p = jnp.arange(lp, dtype=jnp.int32)
    pos_s = PAST_LEN + jnp.arange(x_sample.shape[1], dtype=jnp.int32)
    y_prompt, conv_p, ret_p = trunk(x_prompt, conv0, ret0, pos_p, norm_mix_g, w_in, conv_w,
                                    w_conv_out, w_ret_out, w_o, norm_ffn_g, w_group, b_group,
                                    w_router, b_router, w_gate, w_up, w_down, norm_final_g)
    y_sample, conv_s, ret_s = trunk(x_sample, state_conv, state_ret, pos_s, norm_mix_g, w_in, conv_w,
                                    w_conv_out, w_ret_out, w_o, norm_ffn_g, w_group, b_group,
                                    w_router, b_router, w_gate, w_up, w_down, norm_final_g)
    return (y_prompt, y_sample, conv_p, ret_p, conv_s, ret_s)
```

```python
import functools

import jax
import jax.numpy as jnp
from jax import lax
from jax.experimental import pallas as pl
from jax.experimental.pallas import tpu as pltpu

F32 = jnp.float32
BF16 = jnp.bfloat16

D_MODEL = 1024
BATCH = 8
SEQ = 2048
DEPTH = 2
DEC_BATCH = 128
DEC_SEQ = 4
PAST_LEN = 16384
D_CONV = D_MODEL
CONV_K = 3
RET_HEADS = 4
RET_DK = 256
RET_DV = 512
RET_V_W = RET_HEADS * RET_DV
ROPE_BASE = 10000.0
N_GROUPS = 4
EXPERTS_PER_GROUP = 8
N_EXPERTS = N_GROUPS * EXPERTS_PER_GROUP
TOP_K = 2
D_EXPERT = D_MODEL // 2
NORM_EPS = 1e-6
PROJ_W = 3 * D_CONV + 2 * D_MODEL + 2 * RET_V_W + 2 * D_MODEL

T_PROMPT = BATCH * SEQ
T_SAMPLE = DEC_BATCH * DEC_SEQ
T_ALL = T_PROMPT + T_SAMPLE

COL_CB, COL_CC, COL_CH = 0, D_CONV, 2 * D_CONV
COL_Q = 3 * D_CONV
COL_K = COL_Q + D_MODEL
COL_V = COL_K + D_MODEL
COL_G = COL_V + RET_V_W
COL_GA = COL_G + RET_V_W
COL_GB = COL_GA + D_MODEL

TM = 512
N_TILES = T_ALL // TM
PROMPT_TILES = T_PROMPT // TM
TILES_PER_SEQ = SEQ // TM
PROJ_TN = 1024
RET_CHUNK = 128
SAMPLE_ROWS = 16
HALO = 8
ROUTE_W = 128
MOE_BLK = 256
N_ASSIGN = T_ALL * TOP_K
N_BLOCKS = N_ASSIGN // MOE_BLK + N_EXPERTS
N_SLOTS = N_BLOCKS * MOE_BLK
VMEM_LIMIT = 56 * 1024 * 1024


def _cparams(sem):
    return pltpu.CompilerParams(dimension_semantics=sem, vmem_limit_bytes=VMEM_LIMIT)


def _proj_kernel(x_ref, g_ref, w_ref, o_ref, h_scr):
    j = pl.program_id(1)

    @pl.when(j == 0)
    def _():
        x = x_ref[...]
        ms = jnp.mean(x * x, axis=-1, keepdims=True)
        h_scr[...] = (x * lax.rsqrt(ms + NORM_EPS) * g_ref[...]).astype(BF16)

    o_ref[...] = jnp.dot(h_scr[...], w_ref[j], preferred_element_type=F32)


def _proj(x, g, w3):
    nj = PROJ_W // PROJ_TN
    return pl.pallas_call(
        _proj_kernel,
        grid=(N_TILES, nj),
        in_specs=[
            pl.BlockSpec((TM, D_MODEL), lambda i, j: (i, 0)),
            pl.BlockSpec((1, D_MODEL), lambda i, j: (0, 0)),
            pl.BlockSpec((nj, D_MODEL, PROJ_TN), lambda i, j: (0, 0, 0), pipeline_mode=pl.Buffered(1)),
        ],
        out_specs=pl.BlockSpec((TM, PROJ_TN), lambda i, j: (i, j)),
        out_shape=jax.ShapeDtypeStruct((T_ALL, PROJ_W), F32),
        scratch_shapes=[pltpu.VMEM((TM, D_MODEL), BF16)],
        compiler_params=_cparams(("arbitrary", "arbitrary")),
        name="in_proj",
    )(x, g, w3)


def _conv_kernel(cb_ref, cc_ref, ch_ref, hcc_ref, hch_ref, e1_ref, e2_ref, w_ref, za_ref, tail_ref,
                 us_ref, ext_scr):
    i = pl.program_id(0)
    u = cc_ref[...] * ch_ref[...]
    ext_scr[HALO:, :] = u
    w = w_ref[...]

    def gated_taps(prev2, prev1):
        y = prev2 * w[0:1, :]
        y = y + prev1 * w[1:2, :]
        y = y + u * w[2:3, :]
        return (cb_ref[...] * y).astype(BF16)

    @pl.when(i < PROMPT_TILES)
    def _():
        seq_start = (i % TILES_PER_SEQ) == 0
        ext_scr[0:HALO, :] = jnp.where(seq_start, 0.0, hcc_ref[...] * hch_ref[...])
        za_ref[...] = gated_taps(ext_scr[HALO - 2:HALO - 2 + TM, :], ext_scr[HALO - 1:HALO - 1 + TM, :])
        tail_ref[0] = u[TM - HALO:, :]

    @pl.when(i >= PROMPT_TILES)
    def _():
        ext_scr[0:HALO, :] = jnp.zeros((HALO, D_CONV), F32)
        t = lax.broadcasted_iota(jnp.int32, (TM, 1), 0) % DEC_SEQ
        prev2 = jnp.where(t >= 2, ext_scr[HALO - 2:HALO - 2 + TM, :], e2_ref[...])
        prev1 = jnp.where(t >= 1, ext_scr[HALO - 1:HALO - 1 + TM, :], e1_ref[...])
        za_ref[...] = gated_taps(prev2, prev1)
        us_ref[...] = u


def _conv(proj, conv_w, e1, e2):
    halo_blocks = TM // HALO

    def halo_map(col):
        return lambda i: (jnp.maximum(i * halo_blocks - 1, 0), col)

    const = lambda i: (0, 0)
    return pl.pallas_call(
        _conv_kernel,
        grid=(N_TILES,),
        in_specs=[
            pl.BlockSpec((TM, D_CONV), lambda i: (i, COL_CB // D_CONV)),
            pl.BlockSpec((TM, D_CONV), lambda i: (i, COL_CC // D_CONV)),
            pl.BlockSpec((TM, D_CONV), lambda i: (i, COL_CH // D_CONV)),
            pl.BlockSpec((HALO, D_CONV), halo_map(COL_CC // D_CONV)),
            pl.BlockSpec((HALO, D_CONV), halo_map(COL_CH // D_CONV)),
            pl.BlockSpec((TM, D_CONV), const),
            pl.BlockSpec((TM, D_CONV), const),
            pl.BlockSpec((CONV_K, D_CONV), const),
        ],
        out_specs=[
            pl.BlockSpec((TM, D_CONV), lambda i: (i, 0)),
            pl.BlockSpec((1, HALO, D_CONV), lambda i: (jnp.minimum(i // TILES_PER_SEQ, BATCH - 1), 0, 0)),
            pl.BlockSpec((TM, D_CONV), const),
        ],
        out_shape=[
            jax.ShapeDtypeStruct((T_ALL, D_CONV), BF16),
            jax.ShapeDtypeStruct((BATCH, HALO, D_CONV), F32),
            jax.ShapeDtypeStruct((T_SAMPLE, D_CONV), F32),
        ],
        scratch_shapes=[pltpu.VMEM((HALO + TM, D_CONV), F32)],
        compiler_params=_cparams(("arbitrary",)),
        name="conv_gate",
    )(proj, proj, proj, proj, proj, e1, e2, conv_w)


def _rotate(t, cos, sin):
    half = RET_DK // 2
    t1, t2 = t[:, :half], t[:, half:]
    return jnp.concatenate([t1 * cos - t2 * sin, t1 * sin + t2 * cos], axis=-1)


def _head_norm_gate(o, g):
    o = o * lax.rsqrt(jnp.mean(o * o, axis=-1, keepdims=True) + NORM_EPS)
    return (g * jax.nn.sigmoid(g)) * o


def _ret_prompt_kernel(q_ref, k_ref, v_ref, g_ref, cos_ref, sin_ref, dmat_ref, qdec_ref, kdec_ref,
                       cdec_ref, o_ref, s_out_ref, s_scr):
    c = pl.program_id(2)

    @pl.when(c == 0)
    def _():
        s_scr[...] = jnp.zeros_like(s_scr)

    cos, sin = cos_ref[...], sin_ref[...]
    qr = _rotate(q_ref[...], cos, sin)
    kr = _rotate(k_ref[...], cos, sin) * (RET_DK ** -0.5)
    vb = v_ref[...].astype(BF16)
    scores = lax.dot_general(qr.astype(BF16), kr.astype(BF16), (((1,), (1,)), ((), ())),
                             preferred_element_type=F32) * dmat_ref[0]
    intra = jnp.dot(scores.astype(BF16), vb, preferred_element_type=F32)
    s = s_scr[...]
    cross = jnp.dot((qr * qdec_ref[0]).astype(BF16), s.astype(BF16), preferred_element_type=F32)
    kd = (kr * kdec_ref[0]).astype(BF16)
    s_new = s * cdec_ref[0] + lax.dot_general(kd, vb, (((0,), (0,)), ((), ())),
                                              preferred_element_type=F32)
    s_scr[...] = s_new
    o_ref[...] = _head_norm_gate(intra + cross, g_ref[...]).astype(BF16)

    @pl.when(c == pl.num_programs(2) - 1)
    def _():
        s_out_ref[0, 0] = s_new


def _ret_prompt(proj, tabs):
    nc = SEQ // RET_CHUNK
    qb, kb = COL_Q // RET_DK, COL_K // RET_DK
    vb, gb = COL_V // RET_DV, COL_G // RET_DV
    row = lambda b, h, c: b * nc + c
    return pl.pallas_call(
        _ret_prompt_kernel,
        grid=(BATCH, RET_HEADS, nc),
        in_specs=[
            pl.BlockSpec((RET_CHUNK, RET_DK), lambda b, h, c: (row(b, h, c), qb + h)),
            pl.BlockSpec((RET_CHUNK, RET_DK), lambda b, h, c: (row(b, h, c), kb + h)),
            pl.BlockSpec((RET_CHUNK, RET_DV), lambda b, h, c: (row(b, h, c), vb + h)),
            pl.BlockSpec((RET_CHUNK, RET_DV), lambda b, h, c: (row(b, h, c), gb + h)),
            pl.BlockSpec((RET_CHUNK, RET_DK // 2), lambda b, h, c: (c, 0)),
            pl.BlockSpec((RET_CHUNK, RET_DK // 2), lambda b, h, c: (c, 0)),
            pl.BlockSpec((1, RET_CHUNK, RET_CHUNK), lambda b, h, c: (h, 0, 0)),
            pl.BlockSpec((1, RET_CHUNK, RET_DK), lambda b, h, c: (h, 0, 0)),
            pl.BlockSpec((1, RET_CHUNK, RET_DK), lambda b, h, c: (h, 0, 0)),
            pl.BlockSpec((1, 1, RET_DV), lambda b, h, c: (h, 0, 0)),
        ],
        out_specs=[
            pl.BlockSpec((RET_CHUNK, RET_DV), lambda b, h, c: (row(b, h, c), h)),
            pl.BlockSpec((1, 1, RET_DK, RET_DV), lambda b, h, c: (b, h, 0, 0)),
        ],
        out_shape=[
            jax.ShapeDtypeStruct((T_PROMPT, RET_V_W), BF16),
            jax.ShapeDtypeStruct((BATCH, RET_HEADS, RET_DK, RET_DV), F32),
        ],
        scratch_shapes=[pltpu.VMEM((RET_DK, RET_DV), F32)],
        compiler_params=_cparams(("arbitrary", "arbitrary", "arbitrary")),
        name="ret_prompt",
    )(proj, proj, proj, proj, tabs["cos"], tabs["sin"], tabs["dmat"], tabs["qdec"], tabs["kdec"],
      tabs["cdec"])


def _ret_sample_kernel(q_ref, k_ref, v_ref, g_ref, cos_ref, sin_ref, dmat_ref, qdec_ref, kdec_ref,
                       cdec_ref, s_ref, o_ref, s_out_ref):
    cos, sin = cos_ref[...], sin_ref[...]
    qr = _rotate(q_ref[...], cos, sin)
    kr = _rotate(k_ref[...], cos, sin) * (RET_DK ** -0.5)
    vb = v_ref[...].astype(BF16)
    scores = lax.dot_general(qr.astype(BF16), kr.astype(BF16), (((1,), (1,)), ((), ())),
                             preferred_element_type=F32) * dmat_ref[0]
    o = jnp.dot(scores.astype(BF16), vb, preferred_element_type=F32)
    qd = (qr * qdec_ref[0]).astype(BF16)
    kd = kr * kdec_ref[0]
    seq_of_row = lax.broadcasted_iota(jnp.int32, (SAMPLE_ROWS, 1), 0) // DEC_SEQ
    cdec = cdec_ref[0]
    for b in range(SAMPLE_ROWS // DEC_SEQ):
        mine = seq_of_row == b
        s = s_ref[b, 0]
        cross = jnp.dot(qd, s.astype(BF16), preferred_element_type=F32)
        o = o + jnp.where(mine, cross, 0.0)
        kd_b = jnp.where(mine, kd, 0.0).astype(BF16)
        s_out_ref[b, 0] = s * cdec + lax.dot_general(kd_b, vb, (((0,), (0,)), ((), ())),
                                                     preferred_element_type=F32)
    o_ref[...] = _head_norm_gate(o, g_ref[...]).astype(BF16)


def _ret_sample(proj, state, tabs):
    nb = SAMPLE_ROWS // DEC_SEQ
    row0 = T_PROMPT // SAMPLE_ROWS
    qb, kb = COL_Q // RET_DK, COL_K // RET_DK
    vb, gb = COL_V // RET_DV, COL_G // RET_DV
    return pl.pallas_call(
        _ret_sample_kernel,
        grid=(T_SAMPLE // SAMPLE_ROWS, RET_HEADS),
        in_specs=[
            pl.BlockSpec((SAMPLE_ROWS, RET_DK), lambda i, h: (row0 + i, qb + h)),
            pl.BlockSpec((SAMPLE_ROWS, RET_DK), lambda i, h: (row0 + i, kb + h)),
            pl.BlockSpec((SAMPLE_ROWS, RET_DV), lambda i, h: (row0 + i, vb + h)),
            pl.BlockSpec((SAMPLE_ROWS, RET_DV), lambda i, h: (row0 + i, gb + h)),
            pl.BlockSpec((SAMPLE_ROWS, RET_DK // 2), lambda i, h: (0, 0)),
            pl.BlockSpec((SAMPLE_ROWS, RET_DK // 2), lambda i, h: (0, 0)),
            pl.BlockSpec((1, SAMPLE_ROWS, SAMPLE_ROWS), lambda i, h: (h, 0, 0)),
            pl.BlockSpec((1, SAMPLE_ROWS, RET_DK), lambda i, h: (h, 0, 0)),
            pl.BlockSpec((1, SAMPLE_ROWS, RET_DK), lambda i, h: (h, 0, 0)),
            pl.BlockSpec((1, 1, RET_DV), lambda i, h: (h, 0, 0)),
            pl.BlockSpec((nb, 1, RET_DK, RET_DV), lambda i, h: (i, h, 0, 0)),
        ],
        out_specs=[
            pl.BlockSpec((SAMPLE_ROWS, RET_DV), lambda i, h: (i, h)),
            pl.BlockSpec((nb, 1, RET_DK, RET_DV), lambda i, h: (i, h, 0, 0)),
        ],
        out_shape=[
            jax.ShapeDtypeStruct((T_SAMPLE, RET_V_W), BF16),
            jax.ShapeDtypeStruct((DEC_BATCH, RET_HEADS, RET_DK, RET_DV), F32),
        ],
        compiler_params=_cparams(("arbitrary", "arbitrary")),
        name="ret_sample",
    )(proj, proj, proj, proj, tabs["cos"], tabs["sin"], tabs["dmat"], tabs["qdec"], tabs["kdec"],
      tabs["cdec"], state)


def _retention_tables(chunk, rows, pos):
    log_g = jnp.log(1.0 - 2.0 ** (-5.0 - jnp.arange(RET_HEADS, dtype=F32)))
    r = jnp.arange(rows)
    idx = (r % chunk).astype(F32)
    rel = idx[:, None] - idx[None, :]
    same = (r[:, None] // chunk) == (r[None, :] // chunk)
    dmat = jnp.where((rel >= 0) & same,
                     jnp.exp(jnp.maximum(rel, 0.0)[None] * log_g[:, None, None]), 0.0)
    q_dec = jnp.exp((idx[None, :] + 1.0) * log_g[:, None])
    k_dec = jnp.exp((chunk - 1.0 - idx[None, :]) * log_g[:, None])
    c_dec = jnp.exp(chunk * log_g)
    half = RET_DK // 2
    inv = ROPE_BASE ** (-jnp.arange(half, dtype=F32) / half)
    ang = pos.astype(F32)[:, None] * inv[None, :]
    return {
        "cos": jnp.cos(ang), "sin": jnp.sin(ang), "dmat": dmat,
        "qdec": jnp.broadcast_to(q_dec[:, :, None], (RET_HEADS, rows, RET_DK)),
        "kdec": jnp.broadcast_to(k_dec[:, :, None], (RET_HEADS, rows, RET_DK)),
        "cdec": jnp.broadcast_to(c_dec[:, None, None], (RET_HEADS, 1, RET_DV)),
    }


def _mix_kernel(za_ref, ogp_ref, ogs_ref, ga_ref, gb_ref, x_ref, wco_ref, wro_ref, wo_ref, nf_ref,
                wrt_ref, brt_ref, xn_ref, h2_ref, eid_ref, gate_ref):
    og = jnp.where(pl.program_id(0) < PROMPT_TILES, ogp_ref[...], ogs_ref[...])
    y_a = jnp.dot(za_ref[...], wco_ref[...], preferred_element_type=F32)
    y_b = jnp.dot(og, wro_ref[...], preferred_element_type=F32)
    merged = jax.nn.sigmoid(ga_ref[...]) * y_a + jax.nn.sigmoid(gb_ref[...]) * y_b
    x = x_ref[...] + jnp.dot(merged.astype(BF16), wo_ref[...], preferred_element_type=F32)
    xn_ref[...] = x
    h2 = x * lax.rsqrt(jnp.mean(x * x, axis=-1, keepdims=True) + NORM_EPS) * nf_ref[...]
    h2_ref[...] = h2
    logits = jnp.dot(h2, wrt_ref[...], preferred_element_type=F32,
                     precision=lax.Precision.HIGHEST) + brt_ref[...]
    lane = lax.broadcasted_iota(jnp.int32, (TM, ROUTE_W), 1)
    neg = -jnp.inf
    is_group = lane < N_GROUPS
    gl = jnp.where(is_group, logits, neg)
    gmax = jnp.max(gl, axis=-1, keepdims=True)
    g_top = jnp.min(jnp.where(gl == gmax, lane, ROUTE_W), axis=-1, keepdims=True)
    p_group = 1.0 / jnp.sum(jnp.where(is_group, jnp.exp(gl - gmax), 0.0), axis=-1, keepdims=True)
    lo = N_GROUPS + g_top * EXPERTS_PER_GROUP
    el = jnp.where(lane >= lo, jnp.where(lane < lo + EXPERTS_PER_GROUP, logits, neg), neg)
    v1 = jnp.max(el, axis=-1, keepdims=True)
    i1 = jnp.min(jnp.where(el == v1, lane, ROUTE_W), axis=-1, keepdims=True)
    el2 = jnp.where(lane == i1, neg, el)
    v2 = jnp.max(el2, axis=-1, keepdims=True)
    i2 = jnp.min(jnp.where(el2 == v2, lane, ROUTE_W), axis=-1, keepdims=True)
    e21 = jnp.exp(v2 - v1)
    denom = 1.0 + e21
    eid_ref[...] = jnp.where(lane == 0, i1 - N_GROUPS, jnp.where(lane == 1, i2 - N_GROUPS, 0))
    gate_ref[...] = jnp.where(lane == 0, p_group * (1.0 / denom),
                              jnp.where(lane == 1, p_group * (e21 / denom), 0.0))


def _mix(za, og_p, og_s, proj, x, wco, wro, wo, nf, wrt, brt):
    const = lambda i: (0, 0)
    return pl.pallas_call(
        _mix_kernel,
        grid=(N_TILES,),
        in_specs=[
            pl.BlockSpec((TM, D_CONV), lambda i: (i, 0)),
            pl.BlockSpec((TM, RET_V_W), lambda i: (jnp.minimum(i, PROMPT_TILES - 1), 0)),
            pl.BlockSpec((TM, RET_V_W), const),
            pl.BlockSpec((TM, D_MODEL), lambda i: (i, COL_GA // D_MODEL)),
            pl.BlockSpec((TM, D_MODEL), lambda i: (i, COL_GB // D_MODEL)),
            pl.BlockSpec((TM, D_MODEL), lambda i: (i, 0)),
            pl.BlockSpec((D_CONV, D_MODEL), const),
            pl.BlockSpec((RET_V_W, D_MODEL), const),
            pl.BlockSpec((D_MODEL, D_MODEL), const),
            pl.BlockSpec((1, D_MODEL), const),
            pl.BlockSpec((D_MODEL, ROUTE_W), const),
            pl.BlockSpec((1, ROUTE_W), const),
        ],
        out_specs=[
            pl.BlockSpec((TM, D_MODEL), lambda i: (i, 0)),
            pl.BlockSpec((TM, D_MODEL), lambda i: (i, 0)),
            pl.BlockSpec((TM, ROUTE_W), lambda i: (i, 0)),
            pl.BlockSpec((TM, ROUTE_W), lambda i: (i, 0)),
        ],
        out_shape=[
            jax.ShapeDtypeStruct((T_ALL, D_MODEL), F32),
            jax.ShapeDtypeStruct((T_ALL, D_MODEL), F32),
            jax.ShapeDtypeStruct((T_ALL, ROUTE_W), jnp.int32),
            jax.ShapeDtypeStruct((T_ALL, ROUTE_W), F32),
        ],
        compiler_params=_cparams(("arbitrary",)),
        name="mix_route",
    )(za, og_p, og_s, proj, proj, x, wco, wro, wo, nf, wrt, brt)


def _dispatch(eid):
    flat_e = eid.reshape(N_ASSIGN)
    onehot = (flat_e[:, None] == jnp.arange(N_EXPERTS, dtype=jnp.int32)[None, :]).astype(jnp.int32)
    csum = jnp.cumsum(onehot, axis=0)
    rank = jnp.take_along_axis(csum, flat_e[:, None], axis=1)[:, 0] - 1
    counts = csum[-1]
    padded = (counts + MOE_BLK - 1) // MOE_BLK * MOE_BLK
    end_pad = jnp.cumsum(padded)
    start_pad = end_pad - padded
    slot = start_pad[flat_e] + rank
    slot_assign = jnp.full((N_SLOTS,), -1, jnp.int32).at[slot].set(jnp.arange(N_ASSIGN, dtype=jnp.int32))
    block_start = jnp.arange(N_BLOCKS, dtype=jnp.int32) * MOE_BLK
    block_expert = jnp.minimum(jnp.searchsorted(end_pad, block_start, side="right"),
                               N_EXPERTS - 1).astype(jnp.int32)
    n_valid = jnp.clip(counts[block_expert] - (block_start - start_pad[block_expert]), 0, MOE_BLK)
    return block_expert, n_valid.astype(jnp.int32), slot_assign


def _expert_kernel(be_ref, nv_ref, sa_ref, h_hbm, wg_ref, wu_ref, wd_ref, y_hbm, xbuf, ybuf, gsem, ssem):
    del be_ref
    i = pl.program_id(0)
    n = pl.num_programs(0)

    def gather_copy(blk, s, slot):
        a = sa_ref[blk * MOE_BLK + s]
        return pltpu.make_async_copy(h_hbm.at[pl.ds(a // TOP_K, 1)], xbuf.at[slot, pl.ds(s, 1)],
                                     gsem.at[slot])

    def scatter_copy(blk, s, slot):
        a = sa_ref[blk * MOE_BLK + s]
        row = (a % TOP_K) * T_ALL + a // TOP_K
        return pltpu.make_async_copy(ybuf.at[slot, pl.ds(s, 1)], y_hbm.at[pl.ds(row, 1)], ssem.at[slot])

    def for_valid(blk, fn):
        lax.fori_loop(0, nv_ref[blk], lambda s, c: (fn(s), c)[1], 0)

    @pl.when(i == 0)
    def _():
        xbuf[...] = jnp.zeros_like(xbuf)
        for_valid(0, lambda s: gather_copy(0, s, 0).start())

    @pl.when(i + 1 < n)
    def _():
        for_valid(i + 1, lambda s: gather_copy(i + 1, s, (i + 1) % 2).start())

    slot = i % 2
    for_valid(i, lambda s: gather_copy(i, s, slot).wait())

    @pl.when(i >= 2)
    def _():
        for_valid(i - 2, lambda s: scatter_copy(i - 2, s, slot).wait())

    @pl.when(nv_ref[i] > 0)
    def _():
        xb = xbuf[slot].astype(BF16)
        a = jnp.dot(xb, wg_ref[0], preferred_element_type=F32)
        b = jnp.dot(xb, wu_ref[0], preferred_element_type=F32)
        hmid = (a * jax.nn.sigmoid(a)) * b
        ybuf[slot] = jnp.dot(hmid.astype(BF16), wd_ref[0], preferred_element_type=F32)

    for_valid(i, lambda s: scatter_copy(i, s, slot).start())

    @pl.when(i == n - 1)
    def _():
        @pl.when(i >= 1)
        def _():
            for_valid(i - 1, lambda s: scatter_copy(i - 1, s, 1 - slot).wait())

        for_valid(i, lambda s: scatter_copy(i, s, slot).wait())


def _experts(block_expert, n_valid, slot_assign, h2, wg, wu, wd):
    grid_spec = pltpu.PrefetchScalarGridSpec(
        num_scalar_prefetch=3,
        grid=(N_BLOCKS,),
        in_specs=[
            pl.BlockSpec(memory_space=pl.ANY),
            pl.BlockSpec((1, D_MODEL, D_EXPERT), lambda i, be, nv, sa: (be[i], 0, 0)),
            pl.BlockSpec((1, D_MODEL, D_EXPERT), lambda i, be, nv, sa: (be[i], 0, 0)),
            pl.BlockSpec((1, D_EXPERT, D_MODEL), lambda i, be, nv, sa: (be[i], 0, 0)),
        ],
        out_specs=pl.BlockSpec(memory_space=pl.ANY),
        scratch_shapes=[
            pltpu.VMEM((2, MOE_BLK, D_MODEL), F32),
            pltpu.VMEM((2, MOE_BLK, D_MODEL), F32),
            pltpu.SemaphoreType.DMA((2,)),
            pltpu.SemaphoreType.DMA((2,)),
        ],
    )
    return pl.pallas_call(
        _expert_kernel,
        grid_spec=grid_spec,
        out_shape=jax.ShapeDtypeStruct((TOP_K * T_ALL, D_MODEL), F32),
        compiler_params=pltpu.CompilerParams(dimension_semantics=("arbitrary",),
                                             vmem_limit_bytes=VMEM_LIMIT, has_side_effects=True),
        name="moe_experts",
    )(block_expert, n_valid, slot_assign, h2, wg, wu, wd)


def _combine_kernel(x_ref, y0_ref, y1_ref, gate_ref, nf_ref, o_ref, *, final_norm):
    gate = gate_ref[...]
    x = x_ref[...] + (y0_ref[...] * gate[:, 0:1] + y1_ref[...] * gate[:, 1:2])
    if final_norm:
        x = x * lax.rsqrt(jnp.mean(x * x, axis=-1, keepdims=True) + NORM_EPS) * nf_ref[...]
    o_ref[...] = x


def _combine(x, y, gate, nf, final_norm):
    return pl.pallas_call(
        functools.partial(_combine_kernel, final_norm=final_norm),
        grid=(N_TILES,),
        in_specs=[
            pl.BlockSpec((TM, D_MODEL), lambda i: (i, 0)),
            pl.BlockSpec((TM, D_MODEL), lambda i: (i, 0)),
            pl.BlockSpec((TM, D_MODEL), lambda i: (N_TILES + i, 0)),
            pl.BlockSpec((TM, ROUTE_W), lambda i: (i, 0)),
            pl.BlockSpec((1, D_MODEL), lambda i: (0, 0)),
        ],
        out_specs=pl.BlockSpec((TM, D_MODEL), lambda i: (i, 0)),
        out_shape=jax.ShapeDtypeStruct((T_ALL, D_MODEL), F32),
        compiler_params=_cparams(("arbitrary",)),
        name="moe_combine",
    )(x, y, y, gate, nf)


def kernel(x_prompt, x_sample, state_conv, state_ret, norm_mix_g, w_in, conv_w, w_conv_out, w_ret_out,
           w_o, norm_ffn_g, w_group, b_group, w_router, b_router, w_gate, w_up, w_down, norm_final_g):
    x = jnp.concatenate([x_prompt.reshape(T_PROMPT, D_MODEL), x_sample.reshape(T_SAMPLE, D_MODEL)], axis=0)
    nj = PROJ_W // PROJ_TN
    w_in3 = w_in.astype(BF16).reshape(DEPTH, D_MODEL, nj, PROJ_TN).transpose(0, 2, 1, 3)
    wco, wro, wo = w_conv_out.astype(BF16), w_ret_out.astype(BF16), w_o.astype(BF16)
    wg, wu, wd = w_gate.astype(BF16), w_up.astype(BF16), w_down.astype(BF16)
    pad = ROUTE_W - N_GROUPS - N_EXPERTS
    w_route = jnp.pad(jnp.concatenate([w_group, w_router], axis=-1), ((0, 0), (0, 0), (0, pad)))
    b_route = jnp.pad(jnp.concatenate([b_group, b_router], axis=-1), ((0, 0), (0, pad)))
    tabs_p = _retention_tables(RET_CHUNK, RET_CHUNK, jnp.arange(SEQ, dtype=jnp.int32))
    tabs_s = _retention_tables(DEC_SEQ, SAMPLE_ROWS,
                               PAST_LEN + jnp.arange(SAMPLE_ROWS, dtype=jnp.int32) % DEC_SEQ)
    zeros = jnp.zeros((DEPTH, DEC_BATCH, 1, D_CONV), F32)
    s0, s1 = state_conv[:, :, 0:1], state_conv[:, :, 1:2]
    e1 = jnp.concatenate([s1, zeros, zeros, zeros], axis=2).reshape(DEPTH, T_SAMPLE, D_CONV)
    e2 = jnp.concatenate([s0, s1, zeros, zeros], axis=2).reshape(DEPTH, T_SAMPLE, D_CONV)

    conv_p, conv_s, ret_p, ret_s = [], [], [], []
    for l in range(DEPTH):
        proj = _proj(x, norm_mix_g[l][None, :], w_in3[l])
        za, tail, u_s = _conv(proj, conv_w[l], e1[l], e2[l])
        og_p, s_p = _ret_prompt(proj, tabs_p)
        og_s, s_s = _ret_sample(proj, state_ret[l], tabs_s)
        x, h2, eid, gate = _mix(za, og_p, og_s, proj, x, wco[l], wro[l], wo[l], norm_ffn_g[l][None, :],
                                w_route[l], b_route[l][None, :])
        block_expert, n_valid, slot_assign = _dispatch(eid[:, :TOP_K])
        y = _experts(block_expert, n_valid, slot_assign, h2, wg[l], wu[l], wd[l])
        x = _combine(x, y, gate, norm_final_g[None, :], final_norm=(l == DEPTH - 1))
        conv_p.append(tail[:, HALO - (CONV_K - 1):, :])
        conv_s.append(u_s.reshape(DEC_BATCH, DEC_SEQ, D_CONV)[:, DEC_SEQ - (CONV_K - 1):, :])
        ret_p.append(s_p)
        ret_s.append(s_s)
    y_prompt = x[:T_PROMPT].reshape(BATCH, SEQ, D_MODEL)
    y_sample = x[T_PROMPT:].reshape(DEC_BATCH, DEC_SEQ, D_MODEL)
    return (y_prompt, y_sample, jnp.stack(conv_p), jnp.stack(ret_p), jnp.stack(conv_s), jnp.stack(ret_s))
```
